```python
import math
import jax, jax.numpy as jnp
from jax import lax
import numpy as np

D_MODEL = 1024
BATCH = 32
SEQ = 2048
DEPTH = 4

N_MIXERS = 2
EPS = 1e-6
BLOCK = 128
RET_HEADS = D_MODEL // 256
RET_DK = 256
RET_DV = 2 * RET_DK
RET_QK = RET_HEADS * RET_DK
RET_V = RET_HEADS * RET_DV
RET_IN = 2 * RET_QK + 2 * RET_V
ROPE_BASE = 10000.0
SB_HEADS = D_MODEL // 64
SB_DH = 64
SB_IN = 3 * SB_HEADS * SB_DH
D_FF = 2816
CONV_W = 3

kernel_name = "hybrid_retention_stickbreaking_convffn"


def _rmsnorm(x, g):
    xf = x.astype(jnp.float32)
    y = xf * lax.rsqrt(jnp.mean(xf * xf, axis=-1, keepdims=True) + EPS)
    return (y * g.astype(jnp.float32)).astype(x.dtype)


def _rotary(x, cos, sin):
    x1, x2 = jnp.split(x, 2, axis=-1)
    c = cos[None, :, None, :]
    s = sin[None, :, None, :]
    return jnp.concatenate([x1 * c - x2 * s, x1 * s + x2 * c], axis=-1)


def _retention(h, w_in, w_out):
    B_, S_, _ = h.shape
    nc = S_ // BLOCK
    proj = h @ w_in
    q, k, v, g = jnp.split(proj, [RET_QK, 2 * RET_QK, 2 * RET_QK + RET_V], axis=-1)
    pos = jnp.arange(S_, dtype=jnp.float32)
    inv_freq = ROPE_BASE ** (-jnp.arange(0, RET_DK, 2, dtype=jnp.float32) / RET_DK)
    ang = pos[:, None] * inv_freq[None, :]
    cos, sin = jnp.cos(ang), jnp.sin(ang)
    q = _rotary(q.reshape(B_, S_, RET_HEADS, RET_DK).astype(jnp.float32), cos, sin)
    k = _rotary(k.reshape(B_, S_, RET_HEADS, RET_DK).astype(jnp.float32), cos, sin) * (RET_DK ** -0.5)
    v = v.reshape(B_, S_, RET_HEADS, RET_DV).astype(jnp.float32)
    qc = q.transpose(0, 2, 1, 3).reshape(B_, RET_HEADS, nc, BLOCK, RET_DK)
    kc = k.transpose(0, 2, 1, 3).reshape(B_, RET_HEADS, nc, BLOCK, RET_DK)
    vc = v.transpose(0, 2, 1, 3).reshape(B_, RET_HEADS, nc, BLOCK, RET_DV)
    log_g = jnp.log(1.0 - 2.0 ** (-5.0 - jnp.arange(RET_HEADS, dtype=jnp.float32)))
    cpos = jnp.arange(BLOCK, dtype=jnp.float32)
    diff = cpos[:, None] - cpos[None, :]
    dmat = jnp.where(diff[None] >= 0,
                     jnp.exp(jnp.maximum(diff, 0.0)[None] * log_g[:, None, None]), 0.0)
    scores = jnp.einsum('bhncd,bhnkd->bhnck', qc, kc) * dmat[None, :, None]
    intra = jnp.einsum('bhnck,bhnke->bhnce', scores, vc)
    q_dec = qc * jnp.exp((cpos + 1.0)[None, :] * log_g[:, None])[None, :, None, :, None]
    k_dec = kc * jnp.exp((BLOCK - 1.0 - cpos)[None, :] * log_g[:, None])[None, :, None, :, None]
    chunk_decay = jnp.exp(BLOCK * log_g)[None, :, None, None]

    def step(state, xs):
        qd, kd, vv = xs
        out = jnp.einsum('bhcd,bhde->bhce', qd, state)
        state = chunk_decay * state + jnp.einsum('bhkd,bhke->bhde', kd, vv)
        return state, out

    state0 = jnp.zeros((B_, RET_HEADS, RET_DK, RET_DV), jnp.float32)
    _, cross = lax.scan(step, state0, (jnp.moveaxis(q_dec, 2, 0), jnp.moveaxis(k_dec, 2, 0),
                                       jnp.moveaxis(vc, 2, 0)))
    o = intra + jnp.moveaxis(cross, 0, 2)
    o = o.reshape(B_, RET_HEADS, S_, RET_DV)
    o = o * lax.rsqrt(jnp.mean(o * o, axis=-1, keepdims=True) + EPS)
    o = o.transpose(0, 2, 1, 3).reshape(B_, S_, RET_V)
    y = jax.nn.silu(g.astype(jnp.float32)) * o
    return y.astype(h.dtype) @ w_out


def _stick_breaking(h, w_in, w_out):
    B_, S_, _ = h.shape
    nb = S_ // BLOCK
    scale = 1.0 / math.sqrt(SB_DH)
    proj = h @ w_in
    q, k, v = jnp.split(proj, 3, axis=-1)
    to_heads = lambda a: a.reshape(B_, S_, SB_HEADS, SB_DH).transpose(0, 2, 1, 3).astype(jnp.float32)
    q, k, v = to_heads(q), to_heads(k), to_heads(v)
    outs = []
    for i in range(nb):
        qb = q[:, :, i * BLOCK:(i + 1) * BLOCK]
        t_pos = i * BLOCK + jnp.arange(BLOCK)
        kp = jnp.moveaxis(k[:, :, :(i + 1) * BLOCK].reshape(B_, SB_HEADS, i + 1, BLOCK, SB_DH), 2, 0)
        vp = jnp.moveaxis(v[:, :, :(i + 1) * BLOCK].reshape(B_, SB_HEADS, i + 1, BLOCK, SB_DH), 2, 0)

        def step(carry, xs, qb=qb, t_pos=t_pos):
            suffix, acc = carry
            kb, vb, j = xs
            z = jnp.einsum('bhqd,bhkd->bhqk', qb, kb) * scale
            s_pos = j * BLOCK + jnp.arange(BLOCK)
            mask = s_pos[None, :] < t_pos[:, None]
            log1m = jnp.where(mask, jax.nn.log_sigmoid(-z), 0.0)
            excl = lax.cumsum(log1m, axis=3, reverse=True) - log1m
            log_a = jax.nn.log_sigmoid(z) + excl + suffix[..., None]
            a = jnp.where(mask, jnp.exp(log_a), 0.0)
            acc = acc + jnp.einsum('bhqk,bhkd->bhqd', a, vb)
            suffix = suffix + jnp.sum(log1m, axis=3)
            return (suffix, acc), None

        init = (jnp.zeros((B_, SB_HEADS, BLOCK), jnp.float32),
                jnp.zeros((B_, SB_HEADS, BLOCK, SB_DH), jnp.float32))
        (_, acc), _ = lax.scan(step, init, (kp, vp, jnp.arange(i + 1)), reverse=True)
        outs.append(acc)
    o = jnp.concatenate(outs, axis=2)
    o = o.transpose(0, 2, 1, 3).reshape(B_, S_, SB_HEADS * SB_DH)
    return o.astype(h.dtype) @ w_out


def _conv_ffn(h, w_up, conv_w, conv_b, w_down):
    u = h @ w_up
    u = lax.conv_general_dilated(u, conv_w[:, None, :].astype(u.dtype), window_strides=(1,),
                                 padding=[(CONV_W - 1, 0)],
                                 dimension_numbers=('NWC', 'WIO', 'NWC'),
                                 feature_group_count=2 * D_FF) + conv_b
    gate, val = jnp.split(u, 2, axis=-1)
    return (jax.nn.silu(gate) * val) @ w_down


def setup_inputs(seed: int = 0) -> dict:
    key = jax.random.key(seed)
    ks = jax.random.split(key, 16)
    n_ret = (DEPTH + 1) // 2
    n_sb = DEPTH // 2
    nrm = lambda k, shape, fan_in: jax.random.normal(k, shape, jnp.float32) * fan_in ** -0.5
    gain = lambda k, shape: 1.0 + 0.02 * jax.random.normal(k, shape, jnp.float32)
    return {
        "x": jax.random.normal(ks[0], (BATCH, SEQ, D_MODEL), jnp.float32),
        "ret_norm": gain(ks[1], (n_ret, D_MODEL)),
        "ret_w_in": nrm(ks[2], (n_ret, D_MODEL, RET_IN), D_MODEL),
        "ret_w_out": nrm(ks[3], (n_ret, RET_V, D_MODEL), RET_V),
        "sb_norm": gain(ks[4], (n_sb, D_MODEL)),
        "sb_w_in": nrm(ks[5], (n_sb, D_MODEL, SB_IN), D_MODEL),
        "sb_w_out": nrm(ks[6], (n_sb, SB_HEADS * SB_DH, D_MODEL), SB_HEADS * SB_DH),
        "ffn_norm": gain(ks[7], (DEPTH, D_MODEL)),
        "ffn_w_up": nrm(ks[8], (DEPTH, D_MODEL, 2 * D_FF), D_MODEL),
        "ffn_conv_w": nrm(ks[9], (DEPTH, CONV_W, 2 * D_FF), CONV_W),
        "ffn_conv_b": 0.02 * jax.random.normal(ks[10], (DEPTH, 2 * D_FF), jnp.float32),
        "ffn_w_down": nrm(ks[11], (DEPTH, D_FF, D_MODEL), D_FF),
        "final_norm": gain(ks[12], (D_MODEL,)),
    }


def reference(x, ret_norm, ret_w_in, ret_w_out, sb_norm, sb_w_in, sb_w_out,
              ffn_norm, ffn_w_up, ffn_conv_w, ffn_conv_b, ffn_w_down, final_norm):
    for i in range(DEPTH):
        j = i // N_MIXERS
        if i % N_MIXERS == 0:
            x = x + _retention(_rmsnorm(x, ret_norm[j]), ret_w_in[j], ret_w_out[j])
        else:
            x = x + _stick_breaking(_rmsnorm(x, sb_norm[j]), sb_w_in[j], sb_w_out[j])
        x = x + _conv_ffn(_rmsnorm(x, ffn_norm[i]), ffn_w_up[i], ffn_conv_w[i],
                          ffn_conv_b[i], ffn_w_down[i])
    return _rmsnorm(x, final_norm)
```

```python
import functools
import math

import jax
import jax.numpy as jnp
from jax import lax
from jax.experimental import pallas as pl
from jax.experimental.pallas import tpu as pltpu

F32 = jnp.float32
BF16 = jnp.bfloat16

D_MODEL = 1024
DEPTH = 4
EPS = 1e-6
CHUNK = 128
RET_HEADS = 4
RET_DK = 256
RET_DV = 512
RET_QK = RET_HEADS * RET_DK
RET_V = RET_HEADS * RET_DV
RET_IN = 2 * RET_QK + 2 * RET_V
ROPE_BASE = 10000.0
SB_HEADS = 16
SB_DH = 64
SB_IN = 3 * SB_HEADS * SB_DH
D_FF = 2816
CONV_W = 3

LANES = 128
VMEM_LIMIT_BYTES = 56 * 1024 * 1024
ROW_TILE = 512
N_CHUNK = 512
FF_TILE = 256
HALO = 16


def _resident(shape):
    nd = len(shape)
    return pl.BlockSpec(shape, lambda *_: (0,) * nd, pipeline_mode=pl.Buffered(1))


def _rmsnorm_f32(x, g):
    ms = jnp.mean(x * x, axis=-1, keepdims=True)
    return x * lax.rsqrt(ms + EPS) * g


def _norm_matmul_kernel(x_ref, g_ref, w_ref, o_ref):
    xn = _rmsnorm_f32(x_ref[...], g_ref[...]).astype(BF16)
    n_out = o_ref.shape[1]
    for n0 in range(0, n_out, N_CHUNK):
        o_ref[:, n0:n0 + N_CHUNK] = jnp.dot(
            xn, w_ref[:, n0:n0 + N_CHUNK], preferred_element_type=F32).astype(o_ref.dtype)


def _norm_matmul(x, g, w):
    t, d = x.shape
    n = w.shape[1]
    return pl.pallas_call(
        _norm_matmul_kernel,
        grid=(t // ROW_TILE,),
        in_specs=[pl.BlockSpec((ROW_TILE, d), lambda i: (i, 0)),
                  _resident((1, d)),
                  _resident((d, n))],
        out_specs=pl.BlockSpec((ROW_TILE, n), lambda i: (i, 0)),
        out_shape=jax.ShapeDtypeStruct((t, n), BF16),
        compiler_params=pltpu.CompilerParams(
            dimension_semantics=("arbitrary",), vmem_limit_bytes=VMEM_LIMIT_BYTES),
        name="norm_matmul",
    )(x, g.reshape(1, d), w)


def _matmul_residual_kernel(y_ref, w_ref, x_ref, o_ref):
    o_ref[...] = x_ref[...] + jnp.dot(y_ref[...], w_ref[...], preferred_element_type=F32)


def _matmul_residual(y, w, x):
    t, k = y.shape
    d = w.shape[1]
    return pl.pallas_call(
        _matmul_residual_kernel,
        grid=(t // ROW_TILE,),
        in_specs=[pl.BlockSpec((ROW_TILE, k), lambda i: (i, 0)),
                  _resident((k, d)),
                  pl.BlockSpec((ROW_TILE, d), lambda i: (i, 0))],
        out_specs=pl.BlockSpec((ROW_TILE, d), lambda i: (i, 0)),
        out_shape=jax.ShapeDtypeStruct((t, d), F32),
        compiler_params=pltpu.CompilerParams(
            dimension_semantics=("arbitrary",), vmem_limit_bytes=VMEM_LIMIT_BYTES),
        name="matmul_residual",
    )(y, w, x)


def _ffn_kernel(x_ref, halo_ref, g_ref, wup_ref, cw_ref, cb_ref, wdn_ref, fg_ref, o_ref,
                h_scr, *, tiles_per_seq, apply_final):
    i = pl.program_id(0)
    x = x_ref[...]
    g = g_ref[...]
    xn = _rmsnorm_f32(x, g).astype(BF16)
    first = (i % tiles_per_seq) == 0
    hn = _rmsnorm_f32(halo_ref[...], g)
    hn = jnp.where(first, 0.0, hn).astype(BF16)
    xe = jnp.concatenate([hn, xn], axis=0)

    def conv_cols(c0):
        u = jnp.dot(xe, wup_ref[:, c0:c0 + FF_TILE], preferred_element_type=F32)
        w = cw_ref[:, c0:c0 + FF_TILE]
        out = (u[HALO:] * w[2:3]
               + pltpu.roll(u, 1, 0)[HALO:] * w[1:2]
               + pltpu.roll(u, 2, 0)[HALO:] * w[0:1])
        return out + cb_ref[:, c0:c0 + FF_TILE]

    for f0 in range(0, D_FF, FF_TILE):
        gate = conv_cols(f0)
        val = conv_cols(D_FF + f0)
        h_scr[:, f0:f0 + FF_TILE] = (gate * jax.nn.sigmoid(gate) * val).astype(BF16)

    out = x + jnp.dot(h_scr[...], wdn_ref[...], preferred_element_type=F32)
    if apply_final:
        out = _rmsnorm_f32(out, fg_ref[...])
    o_ref[...] = out


def _conv_ffn(x, g, w_up, conv_w, conv_b, w_down, final_g, seq_len, apply_final):
    t, d = x.shape
    tm = ROW_TILE
    kern = functools.partial(_ffn_kernel, tiles_per_seq=seq_len // tm, apply_final=apply_final)
    halo_blocks_per_tile = tm // HALO
    return pl.pallas_call(
        kern,
        grid=(t // tm,),
        in_specs=[pl.BlockSpec((tm, d), lambda i: (i, 0)),
                  pl.BlockSpec((HALO, d),
                               lambda i: (jnp.maximum(i * halo_blocks_per_tile - 1, 0), 0)),
                  _resident((1, d)),
                  _resident((d, 2 * D_FF)),
                  _resident((CONV_W, 2 * D_FF)),
                  _resident((1, 2 * D_FF)),
                  _resident((D_FF, d)),
                  _resident((1, d))],
        out_specs=pl.BlockSpec((tm, d), lambda i: (i, 0)),
        out_shape=jax.ShapeDtypeStruct((t, d), F32),
        scratch_shapes=[pltpu.VMEM((tm, D_FF), BF16)],
        compiler_params=pltpu.CompilerParams(
            dimension_semantics=("arbitrary",), vmem_limit_bytes=VMEM_LIMIT_BYTES),
        name="conv_ffn",
    )(x, x, g.reshape(1, d), w_up, conv_w, conv_b.reshape(1, 2 * D_FF), w_down,
      final_g.reshape(1, d))


RET_ROWS = 512


def _retention_kernel(p_ref, cos_ref, sin_ref, dmat_ref, qdec_ref, kdec_ref, cdec_ref,
                      o_ref, state_scr):
    @pl.when(pl.program_id(1) == 0)
    def _():
        state_scr[...] = jnp.zeros_like(state_scr)

    half = RET_DK // 2

    def rotary(a, cos, sin):
        a1, a2 = a[:, :half], a[:, half:]
        return a1 * cos - a2 * sin, a1 * sin + a2 * cos

    for c in range(RET_ROWS // CHUNK):
        rows = slice(c * CHUNK, (c + 1) * CHUNK)
        cos = cos_ref[rows, :]
        sin = sin_ref[rows, :]
        for h in range(RET_HEADS):
            q = p_ref[rows, h * RET_DK:(h + 1) * RET_DK].astype(F32)
            k = p_ref[rows, RET_QK + h * RET_DK:RET_QK + (h + 1) * RET_DK].astype(F32)
            v = p_ref[rows, 2 * RET_QK + h * RET_DV:2 * RET_QK + (h + 1) * RET_DV]
            gate = p_ref[rows, 2 * RET_QK + RET_V + h * RET_DV:
                         2 * RET_QK + RET_V + (h + 1) * RET_DV].astype(F32)
            q1, q2 = rotary(q, cos, sin)
            k1, k2 = rotary(k, cos, sin)
            kscale = RET_DK ** -0.5
            k1 = k1 * kscale
            k2 = k2 * kscale
            qr = jnp.concatenate([q1, q2], axis=1).astype(BF16)
            kr = jnp.concatenate([k1, k2], axis=1).astype(BF16)
            qd = jnp.concatenate([q1 * qdec_ref[h], q2 * qdec_ref[h]], axis=1).astype(BF16)
            kd = jnp.concatenate([k1 * kdec_ref[h], k2 * kdec_ref[h]], axis=1).astype(BF16)

            scores = lax.dot_general(qr, kr, (((1,), (1,)), ((), ())),
                                     preferred_element_type=F32) * dmat_ref[h]
            st = state_scr[h]
            o = (jnp.dot(scores.astype(BF16), v, preferred_element_type=F32)
                 + jnp.dot(qd, st.astype(BF16), preferred_element_type=F32))
            state_scr[h] = st * cdec_ref[h] + lax.dot_general(
                kd, v, (((0,), (0,)), ((), ())), preferred_element_type=F32)

            o = o * lax.rsqrt(jnp.mean(o * o, axis=-1, keepdims=True) + EPS)
            y = gate * jax.nn.sigmoid(gate) * o
            o_ref[rows, h * RET_DV:(h + 1) * RET_DV] = y.astype(o_ref.dtype)


def _retention_tables(seq_len):
    pos = jnp.arange(seq_len, dtype=F32)
    inv_freq = ROPE_BASE ** (-jnp.arange(0, RET_DK, 2, dtype=F32) / RET_DK)
    ang = pos[:, None] * inv_freq[None, :]
    cos, sin = jnp.cos(ang), jnp.sin(ang)
    log_g = jnp.log(1.0 - 2.0 ** (-5.0 - jnp.arange(RET_HEADS, dtype=F32)))
    cpos = jnp.arange(CHUNK, dtype=F32)
    diff = cpos[:, None] - cpos[None, :]
    dmat = jnp.where(diff[None] >= 0,
                     jnp.exp(jnp.maximum(diff, 0.0)[None] * log_g[:, None, None]), 0.0)
    qdec = jnp.exp((cpos + 1.0)[None, :] * log_g[:, None])
    kdec = jnp.exp((CHUNK - 1.0 - cpos)[None, :] * log_g[:, None])
    cdec = jnp.exp(CHUNK * log_g)
    qdec = jnp.broadcast_to(qdec[:, :, None], (RET_HEADS, CHUNK, RET_DK // 2))
    kdec = jnp.broadcast_to(kdec[:, :, None], (RET_HEADS, CHUNK, RET_DK // 2))
    cdec = jnp.broadcast_to(cdec[:, None, None], (RET_HEADS, 1, RET_DV))
    return cos, sin, dmat, qdec, kdec, cdec


def _retention_core(proj, batch, seq_len):
    t = proj.shape[0]
    cos, sin, dmat, qdec, kdec, cdec = _retention_tables(seq_len)
    nblk = seq_len // RET_ROWS
    return pl.pallas_call(
        _retention_kernel,
        grid=(batch, nblk),
        in_specs=[pl.BlockSpec((RET_ROWS, RET_IN), lambda b, s: (b * nblk + s, 0)),
                  pl.BlockSpec((RET_ROWS, RET_DK // 2), lambda b, s: (s, 0)),
                  pl.BlockSpec((RET_ROWS, RET_DK // 2), lambda b, s: (s, 0)),
                  _resident((RET_HEADS, CHUNK, CHUNK)),
                  _resident((RET_HEADS, CHUNK, RET_DK // 2)),
                  _resident((RET_HEADS, CHUNK, RET_DK // 2)),
                  _resident((RET_HEADS, 1, RET_DV))],
        out_specs=pl.BlockSpec((RET_ROWS, RET_V), lambda b, s: (b * nblk + s, 0)),
        out_shape=jax.ShapeDtypeStruct((t, RET_V), BF16),
        scratch_shapes=[pltpu.VMEM((RET_HEADS, RET_DK, RET_DV), F32)],
        compiler_params=pltpu.CompilerParams(
            dimension_semantics=("arbitrary", "arbitrary"), vmem_limit_bytes=VMEM_LIMIT_BYTES),
        name="retention_core",
    )(proj, cos, sin, dmat, qdec, kdec, cdec)


SB_GROUPS = 1


def _sb_kernel(q_ref, k_ref, v_ref, o_ref, vt_scr, *, seq_len):
    nblk = seq_len // CHUNK
    scale = 1.0 / math.sqrt(SB_DH)
    two = 2 * CHUNK

    for g in range(SB_GROUPS):
        for kb in range(nblk):
            vb = v_ref[kb * CHUNK:(kb + 1) * CHUNK, g * LANES:(g + 1) * LANES]
            vt_scr[g, :, kb * CHUNK:(kb + 1) * CHUNK] = vb.astype(F32).T.astype(BF16)

    row = lax.broadcasted_iota(jnp.int32, (CHUNK, CHUNK), 0)
    col = lax.broadcasted_iota(jnp.int32, (CHUNK, CHUNK), 1)
    tri_incl = (col >= row).astype(BF16)
    krow = lax.broadcasted_iota(jnp.int32, (CHUNK, two), 0)
    qcol = lax.broadcasted_iota(jnp.int32, (CHUNK, two), 1) % CHUNK
    causal = krow < qcol
    upper_rows = row < SB_DH

    def block(g, qm, k0, carry, masked):
        run, acc = carry
        kb = k_ref[pl.ds(k0, CHUNK), g * LANES:(g + 1) * LANES]
        zt = jnp.dot(kb, qm, preferred_element_type=F32)
        sp = jnp.maximum(zt, 0.0) + jnp.log(1.0 + jnp.exp(-jnp.abs(zt)))
        if masked:
            sp = jnp.where(causal, sp, 0.0)
        sp_hi = sp.astype(BF16)
        sp_lo = (sp - sp_hi.astype(F32)).astype(BF16)
        incl = (jnp.dot(tri_incl, sp_hi, preferred_element_type=F32)
                + jnp.dot(tri_incl, sp_lo, preferred_element_type=F32))
        a = jnp.exp(zt - incl - run)
        if masked:
            a = jnp.where(causal, a, 0.0)
        acc = acc + jnp.dot(vt_scr[g, :, pl.ds(k0, CHUNK)], a.astype(BF16),
                            preferred_element_type=F32)
        return run + incl[0:1, :], acc

    def q_body(qi, _):
        q0 = pl.multiple_of(qi * CHUNK, CHUNK)
        for g in range(SB_GROUPS):
            qt = q_ref[pl.ds(q0, CHUNK), g * LANES:(g + 1) * LANES].astype(F32).T * scale
            qm = jnp.concatenate([jnp.where(upper_rows, qt, 0.0),
                                  jnp.where(upper_rows, 0.0, qt)], axis=1).astype(BF16)
            carry = (jnp.zeros((1, two), F32), jnp.zeros((LANES, two), F32))
            carry = block(g, qm, q0, carry, masked=True)

            def k_body(t, carry, g=g, qm=qm):
                k0 = pl.multiple_of((qi - 1 - t) * CHUNK, CHUNK)
                return block(g, qm, k0, carry, masked=False)

            _, acc = lax.fori_loop(0, qi, k_body, carry)
            ot = jnp.concatenate([acc[:SB_DH, :CHUNK], acc[SB_DH:, CHUNK:]], axis=0)
            o_ref[pl.ds(q0, CHUNK), g * LANES:(g + 1) * LANES] = ot.T.astype(o_ref.dtype)
        return 0

    lax.fori_loop(0, nblk, q_body, 0)


def _sb_core(proj, batch, seq_len):
    t = proj.shape[0]
    width = SB_GROUPS * LANES
    n_groups = (SB_HEADS * SB_DH) // width
    kern = functools.partial(_sb_kernel, seq_len=seq_len)
    return pl.pallas_call(
        kern,
        grid=(batch, n_groups),
        in_specs=[pl.BlockSpec((seq_len, width), lambda b, g: (b, g)),
                  pl.BlockSpec((seq_len, width), lambda b, g: (b, n_groups + g)),
                  pl.BlockSpec((seq_len, width), lambda b, g: (b, 2 * n_groups + g))],
        out_specs=pl.BlockSpec((seq_len, width), lambda b, g: (b, g)),
        out_shape=jax.ShapeDtypeStruct((t, SB_HEADS * SB_DH), BF16),
        scratch_shapes=[pltpu.VMEM((SB_GROUPS, LANES, seq_len), BF16)],
        compiler_params=pltpu.CompilerParams(
            dimension_semantics=("arbitrary", "arbitrary"), vmem_limit_bytes=VMEM_LIMIT_BYTES),
        name="sb_core",
    )(proj, proj, proj)


def kernel(x, ret_norm, ret_w_in, ret_w_out, sb_norm, sb_w_in, sb_w_out, ffn_norm,
           ffn_w_up, ffn_conv_w, ffn_conv_b, ffn_w_down, final_norm):
    batch, seq_len, d = x.shape
    assert d == D_MODEL and seq_len % ROW_TILE == 0 and seq_len % RET_ROWS == 0
    xf = x.reshape(batch * seq_len, d)
    for i in range(DEPTH):
        j = i // 2
        if i % 2 == 0:
            proj = _norm_matmul(xf, ret_norm[j], ret_w_in[j].astype(BF16))
            y = _retention_core(proj, batch, seq_len)
            xf = _matmul_residual(y, ret_w_out[j].astype(BF16), xf)
        else:
            proj = _norm_matmul(xf, sb_norm[j], sb_w_in[j].astype(BF16))
            o = _sb_core(proj, batch, seq_len)
            xf = _matmul_residual(o, sb_w_out[j].astype(BF16), xf)
        xf = _conv_ffn(xf, ffn_norm[i], ffn_w_up[i].astype(BF16), ffn_conv_w[i],
                       ffn_conv_b[i], ffn_w_down[i].astype(BF16), final_norm, seq_len,
                       apply_final=(i == DEPTH - 1))
    return xf.reshape(batch, seq_len, d)
```

```python
import functools
import math

import jax
import jax.numpy as jnp
from jax import lax
from jax.experimental import pallas as pl
from jax.experimental.pallas import tpu as pltpu

F32 = jnp.float32
BF16 = jnp.bfloat16

D_MODEL = 1024
DEPTH = 4
EPS = 1e-6
CHUNK = 128
RET_HEADS = 4
RET_DK = 256
RET_DV = 512
RET_QK = RET_HEADS * RET_DK
RET_V = RET_HEADS * RET_DV
RET_IN = 2 * RET_QK + 2 * RET_V
ROPE_BASE = 10000.0
SB_HEADS = 16
SB_DH = 64
SB_IN = 3 * SB_HEADS * SB_DH
D_FF = 2816
CONV_W = 3

LANES = 128
VMEM_LIMIT_BYTES = 56 * 1024 * 1024
ROW_TILE = 512
N_CHUNK = 512
FF_TILE = 256
HALO = 16


def _resident(shape):
    nd = len(shape)
    return pl.BlockSpec(shape, lambda *_: (0,) * nd, pipeline_mode=pl.Buffered(1))


def _rmsnorm_f32(x, g):
    ms = jnp.mean(x * x, axis=-1, keepdims=True)
    return x * lax.rsqrt(ms + EPS) * g


def _norm_matmul_kernel(x_ref, g_ref, w_ref, o_ref):
    xn = _rmsnorm_f32(x_ref[...], g_ref[...]).astype(BF16)
    n_out = o_ref.shape[1]
    for n0 in range(0, n_out, N_CHUNK):
        o_ref[:, n0:n0 + N_CHUNK] = jnp.dot(
            xn, w_ref[:, n0:n0 + N_CHUNK], preferred_element_type=F32).astype(o_ref.dtype)


def _norm_matmul(x, g, w):
    t, d = x.shape
    n = w.shape[1]
    return pl.pallas_call(
        _norm_matmul_kernel,
        grid=(t // ROW_TILE,),
        in_specs=[pl.BlockSpec((ROW_TILE, d), lambda i: (i, 0)),
                  _resident((1, d)),
                  _resident((d, n))],
        out_specs=pl.BlockSpec((ROW_TILE, n), lambda i: (i, 0)),
        out_shape=jax.ShapeDtypeStruct((t, n), BF16),
        compiler_params=pltpu.CompilerParams(
            dimension_semantics=("arbitrary",), vmem_limit_bytes=VMEM_LIMIT_BYTES),
        name="norm_matmul",
    )(x, g.reshape(1, d), w)


def _matmul_residual_kernel(y_ref, w_ref, x_ref, o_ref):
    o_ref[...] = x_ref[...] + jnp.dot(y_ref[...], w_ref[...], preferred_element_type=F32)


def _matmul_residual(y, w, x):
    t, k = y.shape
    d = w.shape[1]
    return pl.pallas_call(
        _matmul_residual_kernel,
        grid=(t // ROW_TILE,),
        in_specs=[pl.BlockSpec((ROW_TILE, k), lambda i: (i, 0)),
                  _resident((k, d)),
                  pl.BlockSpec((ROW_TILE, d), lambda i: (i, 0))],
        out_specs=pl.BlockSpec((ROW_TILE, d), lambda i: (i, 0)),
        out_shape=jax.ShapeDtypeStruct((t, d), F32),
        compiler_params=pltpu.CompilerParams(
            dimension_semantics=("arbitrary",), vmem_limit_bytes=VMEM_LIMIT_BYTES),
        name="matmul_residual",
    )(y, w, x)


def _ffn_kernel(x_ref, halo_ref, g_ref, wup_ref, cw_ref, cb_ref, wdn_ref, fg_ref, o_ref,
                h_scr, *, tiles_per_seq, apply_final):
    i = pl.program_id(0)
    x = x_ref[...]
    g = g_ref[...]
    xn = _rmsnorm_f32(x, g).astype(BF16)
    first = (i % tiles_per_seq) == 0
    hn = _rmsnorm_f32(halo_ref[...], g)
    hn = jnp.where(first, 0.0, hn).astype(BF16)
    xe = jnp.concatenate([hn, xn], axis=0)

    def conv_cols(c0):
        u = jnp.dot(xe, wup_ref[:, c0:c0 + FF_TILE], preferred_element_type=F32)
        w = cw_ref[:, c0:c0 + FF_TILE]
        out = (u[HALO:] * w[2:3]
               + pltpu.roll(u, 1, 0)[HALO:] * w[1:2]
               + pltpu.roll(u, 2, 0)[HALO:] * w[0:1])
        return out + cb_ref[:, c0:c0 + FF_TILE]

    for f0 in range(0, D_FF, FF_TILE):
        gate = conv_cols(f0)
        val = conv_cols(D_FF + f0)
        h_scr[:, f0:f0 + FF_TILE] = (gate * jax.nn.sigmoid(gate) * val).astype(BF16)

    out = x + jnp.dot(h_scr[...], wdn_ref[...], preferred_element_type=F32)
    if apply_final:
        out = _rmsnorm_f32(out, fg_ref[...])
    o_ref[...] = out


def _conv_ffn(x, g, w_up, conv_w, conv_b, w_down, final_g, seq_len, apply_final):
    t, d = x.shape
    tm = ROW_TILE
    kern = functools.partial(_ffn_kernel, tiles_per_seq=seq_len // tm, apply_final=apply_final)
    halo_blocks_per_tile = tm // HALO
    return pl.pallas_call(
        kern,
        grid=(t // tm,),
        in_specs=[pl.BlockSpec((tm, d), lambda i: (i, 0)),
                  pl.BlockSpec((HALO, d),
                               lambda i: (jnp.maximum(i * halo_blocks_per_tile - 1, 0), 0)),
                  _resident((1, d)),
                  _resident((d, 2 * D_FF)),
                  _resident((CONV_W, 2 * D_FF)),
                  _resident((1, 2 * D_FF)),
                  _resident((D_FF, d)),
                  _resident((1, d))],
        out_specs=pl.BlockSpec((tm, d), lambda i: (i, 0)),
        out_shape=jax.ShapeDtypeStruct((t, d), F32),
        scratch_shapes=[pltpu.VMEM((tm, D_FF), BF16)],
        compiler_params=pltpu.CompilerParams(
            dimension_semantics=("arbitrary",), vmem_limit_bytes=VMEM_LIMIT_BYTES),
        name="conv_ffn",
    )(x, x, g.reshape(1, d), w_up, conv_w, conv_b.reshape(1, 2 * D_FF), w_down,
      final_g.reshape(1, d))


RET_ROWS = 512


def _retention_kernel(p_ref, cos_ref, sin_ref, dmat_ref, qdec_ref, kdec_ref, cdec_ref,
                      o_ref, state_scr):
    @pl.when(pl.program_id(1) == 0)
    def _():
        state_scr[...] = jnp.zeros_like(state_scr)

    half = RET_DK // 2

    def rotary(a, cos, sin):
        a1, a2 = a[:, :half], a[:, half:]
        return a1 * cos - a2 * sin, a1 * sin + a2 * cos

    for c in range(RET_ROWS // CHUNK):
        rows = slice(c * CHUNK, (c + 1) * CHUNK)
        cos = cos_ref[rows, :]
        sin = sin_ref[rows, :]
        for h in range(RET_HEADS):
            q = p_ref[rows, h * RET_DK:(h + 1) * RET_DK].astype(F32)
            k = p_ref[rows, RET_QK + h * RET_DK:RET_QK + (h + 1) * RET_DK].astype(F32)
            v = p_ref[rows, 2 * RET_QK + h * RET_DV:2 * RET_QK + (h + 1) * RET_DV]
            gate = p_ref[rows, 2 * RET_QK + RET_V + h * RET_DV:
                         2 * RET_QK + RET_V + (h + 1) * RET_DV].astype(F32)
            q1, q2 = rotary(q, cos, sin)
            k1, k2 = rotary(k, cos, sin)
            kscale = RET_DK ** -0.5
            k1 = k1 * kscale
            k2 = k2 * kscale
            qr = jnp.concatenate([q1, q2], axis=1).astype(BF16)
            kr = jnp.concatenate([k1, k2], axis=1).astype(BF16)
            qd = jnp.concatenate([q1 * qdec_ref[h], q2 * qdec_ref[h]], axis=1).astype(BF16)
            kd = jnp.concatenate([k1 * kdec_ref[h], k2 * kdec_ref[h]], axis=1).astype(BF16)

            scores = lax.dot_general(qr, kr, (((1,), (1,)), ((), ())),
                                     preferred_element_type=F32) * dmat_ref[h]
            st = state_scr[h]
            o = (jnp.dot(scores.astype(BF16), v, preferred_element_type=F32)
                 + jnp.dot(qd, st.astype(BF16), preferred_element_type=F32))
            state_scr[h] = st * cdec_ref[h] + lax.dot_general(
                kd, v, (((0,), (0,)), ((), ())), preferred_element_type=F32)

            o = o * lax.rsqrt(jnp.mean(o * o, axis=-1, keepdims=True) + EPS)
            y = gate * jax.nn.sigmoid(gate) * o
            o_ref[rows, h * RET_DV:(h + 1) * RET_DV] = y.astype(o_ref.dtype)


def _retention_tables(seq_len):
    pos = jnp.arange(seq_len, dtype=F32)
    inv_freq = ROPE_BASE ** (-jnp.arange(0, RET_DK, 2, dtype=F32) / RET_DK)
    ang = pos[:, None] * inv_freq[None, :]
    cos, sin = jnp.cos(ang), jnp.sin(ang)
    log_g = jnp.log(1.0 - 2.0 ** (-5.0 - jnp.arange(RET_HEADS, dtype=F32)))
    cpos = jnp.arange(CHUNK, dtype=F32)
    diff = cpos[:, None] - cpos[None, :]
    dmat = jnp.where(diff[None] >= 0,
                     jnp.exp(jnp.maximum(diff, 0.0)[None] * log_g[:, None, None]), 0.0)
    qdec = jnp.exp((cpos + 1.0)[None, :] * log_g[:, None])
    kdec = jnp.exp((CHUNK - 1.0 - cpos)[None, :] * log_g[:, None])
    cdec = jnp.exp(CHUNK * log_g)
    qdec = jnp.broadcast_to(qdec[:, :, None], (RET_HEADS, CHUNK, RET_DK // 2))
    kdec = jnp.broadcast_to(kdec[:, :, None], (RET_HEADS, CHUNK, RET_DK // 2))
    cdec = jnp.broadcast_to(cdec[:, None, None], (RET_HEADS, 1, RET_DV))
    return cos, sin, dmat, qdec, kdec, cdec


def _retention_core(proj, batch, seq_len):
    t = proj.shape[0]
    cos, sin, dmat, qdec, kdec, cdec = _retention_tables(seq_len)
    nblk = seq_len // RET_ROWS
    return pl.pallas_call(
        _retention_kernel,
        grid=(batch, nblk),
        in_specs=[pl.BlockSpec((RET_ROWS, RET_IN), lambda b, s: (b * nblk + s, 0)),
                  pl.BlockSpec((RET_ROWS, RET_DK // 2), lambda b, s: (s, 0)),
                  pl.BlockSpec((RET_ROWS, RET_DK // 2), lambda b, s: (s, 0)),
                  _resident((RET_HEADS, CHUNK, CHUNK)),
                  _resident((RET_HEADS, CHUNK, RET_DK // 2)),
                  _resident((RET_HEADS, CHUNK, RET_DK // 2)),
                  _resident((RET_HEADS, 1, RET_DV))],
        out_specs=pl.BlockSpec((RET_ROWS, RET_V), lambda b, s: (b * nblk + s, 0)),
        out_shape=jax.ShapeDtypeStruct((t, RET_V), BF16),
        scratch_shapes=[pltpu.VMEM((RET_HEADS, RET_DK, RET_DV), F32)],
        compiler_params=pltpu.CompilerParams(
            dimension_semantics=("arbitrary", "arbitrary"), vmem_limit_bytes=VMEM_LIMIT_BYTES),
        name="retention_core",
    )(proj, cos, sin, dmat, qdec, kdec, cdec)


SB_GROUPS = 8


def _sb_kernel(q_ref, k_ref, v_ref, o_ref, vt_scr, qm_scr, run_scr, acc_scr, *, seq_len):
    nblk = seq_len // CHUNK
    scale = 1.0 / math.sqrt(SB_DH)
    two = 2 * CHUNK

    for g in range(SB_GROUPS):
        for kb in range(nblk):
            vb = v_ref[kb * CHUNK:(kb + 1) * CHUNK, g * LANES:(g + 1) * LANES]
            vt_scr[g, :, kb * CHUNK:(kb + 1) * CHUNK] = vb.astype(F32).T.astype(BF16)

    row = lax.broadcasted_iota(jnp.int32, (CHUNK, CHUNK), 0)
    col = lax.broadcasted_iota(jnp.int32, (CHUNK, CHUNK), 1)
    tri_incl = (col >= row).astype(BF16)
    krow = lax.broadcasted_iota(jnp.int32, (CHUNK, two), 0)
    qcol = lax.broadcasted_iota(jnp.int32, (CHUNK, two), 1) % CHUNK
    causal = krow < qcol
    upper_rows = row < SB_DH

    groups = range(SB_GROUPS)

    def blocks(k0, first):
        zts = [jnp.dot(k_ref[pl.ds(k0, CHUNK), g * LANES:(g + 1) * LANES], qm_scr[g],
                       preferred_element_type=F32) for g in groups]
        incls = []
        for g in groups:
            zt = zts[g]
            sp = jnp.maximum(zt, 0.0) + jnp.log(1.0 + jnp.exp(-jnp.abs(zt)))
            if first:
                sp = jnp.where(causal, sp, 0.0)
            sp_hi = sp.astype(BF16)
            sp_lo = (sp - sp_hi.astype(F32)).astype(BF16)
            incls.append(jnp.dot(tri_incl, sp_hi, preferred_element_type=F32)
                         + jnp.dot(tri_incl, sp_lo, preferred_element_type=F32))
        pvs = []
        for g in groups:
            if first:
                a = jnp.where(causal, jnp.exp(zts[g] - incls[g]), 0.0)
            else:
                a = jnp.exp(zts[g] - incls[g] - run_scr[g])
            pvs.append(jnp.dot(vt_scr[g, :, pl.ds(k0, CHUNK)], a.astype(BF16),
                               preferred_element_type=F32))
        for g in groups:
            if first:
                run_scr[g] = incls[g][0:1, :]
                acc_scr[g] = pvs[g]
            else:
                run_scr[g] = run_scr[g] + incls[g][0:1, :]
                acc_scr[g] = acc_scr[g] + pvs[g]

    def q_body(qi, _):
        q0 = pl.multiple_of(qi * CHUNK, CHUNK)
        for g in range(SB_GROUPS):
            qt = q_ref[pl.ds(q0, CHUNK), g * LANES:(g + 1) * LANES].astype(F32).T * scale
            qm_scr[g] = jnp.concatenate([jnp.where(upper_rows, qt, 0.0),
                                         jnp.where(upper_rows, 0.0, qt)], axis=1).astype(BF16)
        blocks(q0, first=True)

        def k_body(t, _):
            blocks(pl.multiple_of((qi - 1 - t) * CHUNK, CHUNK), first=False)
            return 0

        lax.fori_loop(0, qi, k_body, 0)
        for g in range(SB_GROUPS):
            acc = acc_scr[g]
            ot = jnp.concatenate([acc[:SB_DH, :CHUNK], acc[SB_DH:, CHUNK:]], axis=0)
            o_ref[pl.ds(q0, CHUNK), g * LANES:(g + 1) * LANES] = ot.T.astype(o_ref.dtype)
        return 0

    lax.fori_loop(0, nblk, q_body, 0)


def _sb_core(proj, batch, seq_len):
    t = proj.shape[0]
    width = SB_GROUPS * LANES
    n_groups = (SB_HEADS * SB_DH) // width
    kern = functools.partial(_sb_kernel, seq_len=seq_len)
    return pl.pallas_call(
        kern,
        grid=(batch, n_groups),
        in_specs=[pl.BlockSpec((seq_len, width), lambda b, g: (b, g)),
                  pl.BlockSpec((seq_len, width), lambda b, g: (b, n_groups + g)),
                  pl.BlockSpec((seq_len, width), lambda b, g: (b, 2 * n_groups + g))],
        out_specs=pl.BlockSpec((seq_len, width), lambda b, g: (b, g)),
        out_shape=jax.ShapeDtypeStruct((t, SB_HEADS * SB_DH), BF16),
        scratch_shapes=[pltpu.VMEM((SB_GROUPS, LANES, seq_len), BF16),
                        pltpu.VMEM((SB_GROUPS, LANES, 2 * CHUNK), BF16),
                        pltpu.VMEM((SB_GROUPS, 1, 2 * CHUNK), F32),
                        pltpu.VMEM((SB_GROUPS, LANES, 2 * CHUNK), F32)],
        compiler_params=pltpu.CompilerParams(
            dimension_semantics=("arbitrary", "arbitrary"), vmem_limit_bytes=VMEM_LIMIT_BYTES),
        name="sb_core",
    )(proj, proj, proj)


def kernel(x, ret_norm, ret_w_in, ret_w_out, sb_norm, sb_w_in, sb_w_out, ffn_norm,
           ffn_w_up, ffn_conv_w, ffn_conv_b, ffn_w_down, final_norm):
    batch, seq_len, d = x.shape
    assert d == D_MODEL and seq_len % ROW_TILE == 0 and seq_len % RET_ROWS == 0
    xf = x.reshape(batch * seq_len, d)
    for i in range(DEPTH):
        j = i // 2
        if i % 2 == 0:
            proj = _norm_matmul(xf, ret_norm[j], ret_w_in[j].astype(BF16))
            y = _retention_core(proj, batch, seq_len)
            xf = _matmul_residual(y, ret_w_out[j].astype(BF16), xf)
        else:
            proj = _norm_matmul(xf, sb_norm[j], sb_w_in[j].astype(BF16))
            o = _sb_core(proj, batch, seq_len)
            xf = _matmul_residual(o, sb_w_out[j].astype(BF16), xf)
        xf = _conv_ffn(xf, ffn_norm[i], ffn_w_up[i].astype(BF16), ffn_conv_w[i],
                       ffn_conv_b[i], ffn_w_down[i].astype(BF16), final_norm, seq_len,
                       apply_final=(i == DEPTH - 1))
    return xf.reshape(batch, seq_len, d)
```

```python
import functools
import math

import jax
import jax.numpy as jnp
from jax import lax
from jax.experimental import pallas as pl
from jax.experimental.pallas import tpu as pltpu

F32 = jnp.float32
BF16 = jnp.bfloat16

D_MODEL = 1024
DEPTH = 4
EPS = 1e-6
CHUNK = 128
RET_HEADS = 4
RET_DK = 256
RET_DV = 512
RET_QK = RET_HEADS * RET_DK
RET_V = RET_HEADS * RET_DV
RET_IN = 2 * RET_QK + 2 * RET_V
ROPE_BASE = 10000.0
SB_HEADS = 16
SB_DH = 64
SB_IN = 3 * SB_HEADS * SB_DH
D_FF = 2816
CONV_W = 3
LOG2E = 1.4426950408889634

LANES = 128
VMEM_LIMIT_BYTES = 56 * 1024 * 1024
ROW_TILE = 512
N_CHUNK = 512
FF_TILE = 256
HALO = 16


def _resident(shape):
    nd = len(shape)
    return pl.BlockSpec(shape, lambda *_: (0,) * nd, pipeline_mode=pl.Buffered(1))


def _rmsnorm_f32(x, g):
    ms = jnp.mean(x * x, axis=-1, keepdims=True)
    return x * lax.rsqrt(ms + EPS) * g


def _norm_matmul_kernel(x_ref, g_ref, w_ref, o_ref):
    xn = _rmsnorm_f32(x_ref[...], g_ref[...]).astype(BF16)
    n_out = o_ref.shape[1]
    for n0 in range(0, n_out, N_CHUNK):
        o_ref[:, n0:n0 + N_CHUNK] = jnp.dot(
            xn, w_ref[:, n0:n0 + N_CHUNK], preferred_element_type=F32).astype(o_ref.dtype)


def _norm_matmul(x, g, w):
    t, d = x.shape
    n = w.shape[1]
    return pl.pallas_call(
        _norm_matmul_kernel,
        grid=(t // ROW_TILE,),
        in_specs=[pl.BlockSpec((ROW_TILE, d), lambda i: (i, 0)),
                  _resident((1, d)),
                  _resident((d, n))],
        out_specs=pl.BlockSpec((ROW_TILE, n), lambda i: (i, 0)),
        out_shape=jax.ShapeDtypeStruct((t, n), BF16),
        compiler_params=pltpu.CompilerParams(
            dimension_semantics=("arbitrary",), vmem_limit_bytes=VMEM_LIMIT_BYTES),
        name="norm_matmul",
    )(x, g.reshape(1, d), w)


def _matmul_residual_kernel(y_ref, w_ref, x_ref, o_ref):
    o_ref[...] = x_ref[...] + jnp.dot(y_ref[...], w_ref[...], preferred_element_type=F32)


def _matmul_residual(y, w, x):
    t, k = y.shape
    d = w.shape[1]
    return pl.pallas_call(
        _matmul_residual_kernel,
        grid=(t // ROW_TILE,),
        in_specs=[pl.BlockSpec((ROW_TILE, k), lambda i: (i, 0)),
                  _resident((k, d)),
                  pl.BlockSpec((ROW_TILE, d), lambda i: (i, 0))],
        out_specs=pl.BlockSpec((ROW_TILE, d), lambda i: (i, 0)),
        out_shape=jax.ShapeDtypeStruct((t, d), F32),
        compiler_params=pltpu.CompilerParams(
            dimension_semantics=("arbitrary",), vmem_limit_bytes=VMEM_LIMIT_BYTES),
        name="matmul_residual",
    )(y, w, x)


def _ffn_kernel(x_ref, halo_ref, g_ref, wup_ref, cw_ref, cb_ref, wdn_ref, fg_ref, o_ref,
                h_scr, *, tiles_per_seq, apply_final):
    i = pl.program_id(0)
    x = x_ref[...]
    g = g_ref[...]
    xn = _rmsnorm_f32(x, g).astype(BF16)
    first = (i % tiles_per_seq) == 0
    hn = _rmsnorm_f32(halo_ref[...], g)
    hn = jnp.where(first, 0.0, hn).astype(BF16)
    xe = jnp.concatenate([hn, xn], axis=0)

    def conv_cols(c0):
        u = jnp.dot(xe, wup_ref[:, c0:c0 + FF_TILE], preferred_element_type=F32)
        w = cw_ref[:, c0:c0 + FF_TILE]
        out = (u[HALO:] * w[2:3]
               + pltpu.roll(u, 1, 0)[HALO:] * w[1:2]
               + pltpu.roll(u, 2, 0)[HALO:] * w[0:1])
        return out + cb_ref[:, c0:c0 + FF_TILE]

    for f0 in range(0, D_FF, FF_TILE):
        gate = conv_cols(f0)
        val = conv_cols(D_FF + f0)
        h_scr[:, f0:f0 + FF_TILE] = (gate * jax.nn.sigmoid(gate) * val).astype(BF16)

    out = x + jnp.dot(h_scr[...], wdn_ref[...], preferred_element_type=F32)
    if apply_final:
        out = _rmsnorm_f32(out, fg_ref[...])
    o_ref[...] = out


def _conv_ffn(x, g, w_up, conv_w, conv_b, w_down, final_g, seq_len, apply_final):
    t, d = x.shape
    tm = ROW_TILE
    kern = functools.partial(_ffn_kernel, tiles_per_seq=seq_len // tm, apply_final=apply_final)
    halo_blocks_per_tile = tm // HALO
    return pl.pallas_call(
        kern,
        grid=(t // tm,),
        in_specs=[pl.BlockSpec((tm, d), lambda i: (i, 0)),
                  pl.BlockSpec((HALO, d),
                               lambda i: (jnp.maximum(i * halo_blocks_per_tile - 1, 0), 0)),
                  _resident((1, d)),
                  _resident((d, 2 * D_FF)),
                  _resident((CONV_W, 2 * D_FF)),
                  _resident((1, 2 * D_FF)),
                  _resident((D_FF, d)),
                  _resident((1, d))],
        out_specs=pl.BlockSpec((tm, d), lambda i: (i, 0)),
        out_shape=jax.ShapeDtypeStruct((t, d), F32),
        scratch_shapes=[pltpu.VMEM((tm, D_FF), BF16)],
        compiler_params=pltpu.CompilerParams(
            dimension_semantics=("arbitrary",), vmem_limit_bytes=VMEM_LIMIT_BYTES),
        name="conv_ffn",
    )(x, x, g.reshape(1, d), w_up, conv_w, conv_b.reshape(1, 2 * D_FF), w_down,
      final_g.reshape(1, d))


RET_ROWS = 512


def _retention_kernel(p_ref, cos_ref, sin_ref, dmat_ref, qdec_ref, kdec_ref, cdec_ref,
                      o_ref, state_scr):
    @pl.when(pl.program_id(1) == 0)
    def _():
        state_scr[...] = jnp.zeros_like(state_scr)

    half = RET_DK // 2

    def rotary(a, cos, sin):
        a1, a2 = a[:, :half], a[:, half:]
        return a1 * cos - a2 * sin, a1 * sin + a2 * cos

    for c in range(RET_ROWS // CHUNK):
        rows = slice(c * CHUNK, (c + 1) * CHUNK)
        cos = cos_ref[rows, :]
        sin = sin_ref[rows, :]
        for h in range(RET_HEADS):
            q = p_ref[rows, h * RET_DK:(h + 1) * RET_DK].astype(F32)
            k = p_ref[rows, RET_QK + h * RET_DK:RET_QK + (h + 1) * RET_DK].astype(F32)
            v = p_ref[rows, 2 * RET_QK + h * RET_DV:2 * RET_QK + (h + 1) * RET_DV]
            gate = p_ref[rows, 2 * RET_QK + RET_V + h * RET_DV:
                         2 * RET_QK + RET_V + (h + 1) * RET_DV].astype(F32)
            q1, q2 = rotary(q, cos, sin)
            k1, k2 = rotary(k, cos, sin)
            kscale = RET_DK ** -0.5
            k1 = k1 * kscale
            k2 = k2 * kscale
            qr = jnp.concatenate([q1, q2], axis=1).astype(BF16)
            kr = jnp.concatenate([k1, k2], axis=1).astype(BF16)
            qd = jnp.concatenate([q1 * qdec_ref[h], q2 * qdec_ref[h]], axis=1).astype(BF16)
            kd = jnp.concatenate([k1 * kdec_ref[h], k2 * kdec_ref[h]], axis=1).astype(BF16)

            scores = lax.dot_general(qr, kr, (((1,), (1,)), ((), ())),
                                     preferred_element_type=F32) * dmat_ref[h]
            st = state_scr[h]
            o = (jnp.dot(scores.astype(BF16), v, preferred_element_type=F32)
                 + jnp.dot(qd, st.astype(BF16), preferred_element_type=F32))
            state_scr[h] = st * cdec_ref[h] + lax.dot_general(
                kd, v, (((0,), (0,)), ((), ())), preferred_element_type=F32)

            o = o * lax.rsqrt(jnp.mean(o * o, axis=-1, keepdims=True) + EPS)
            y = gate * jax.nn.sigmoid(gate) * o
            o_ref[rows, h * RET_DV:(h + 1) * RET_DV] = y.astype(o_ref.dtype)


def _retention_tables(seq_len):
    pos = jnp.arange(seq_len, dtype=F32)
    inv_freq = ROPE_BASE ** (-jnp.arange(0, RET_DK, 2, dtype=F32) / RET_DK)
    ang = pos[:, None] * inv_freq[None, :]
    cos, sin = jnp.cos(ang), jnp.sin(ang)
    log_g = jnp.log(1.0 - 2.0 ** (-5.0 - jnp.arange(RET_HEADS, dtype=F32)))
    cpos = jnp.arange(CHUNK, dtype=F32)
    diff = cpos[:, None] - cpos[None, :]
    dmat = jnp.where(diff[None] >= 0,
                     jnp.exp(jnp.maximum(diff, 0.0)[None] * log_g[:, None, None]), 0.0)
    qdec = jnp.exp((cpos + 1.0)[None, :] * log_g[:, None])
    kdec = jnp.exp((CHUNK - 1.0 - cpos)[None, :] * log_g[:, None])
    cdec = jnp.exp(CHUNK * log_g)
    qdec = jnp.broadcast_to(qdec[:, :, None], (RET_HEADS, CHUNK, RET_DK // 2))
    kdec = jnp.broadcast_to(kdec[:, :, None], (RET_HEADS, CHUNK, RET_DK // 2))
    cdec = jnp.broadcast_to(cdec[:, None, None], (RET_HEADS, 1, RET_DV))
    return cos, sin, dmat, qdec, kdec, cdec


def _retention_core(proj, batch, seq_len):
    t = proj.shape[0]
    cos, sin, dmat, qdec, kdec, cdec = _retention_tables(seq_len)
    nblk = seq_len // RET_ROWS
    return pl.pallas_call(
        _retention_kernel,
        grid=(batch, nblk),
        in_specs=[pl.BlockSpec((RET_ROWS, RET_IN), lambda b, s: (b * nblk + s, 0)),
                  pl.BlockSpec((RET_ROWS, RET_DK // 2), lambda b, s: (s, 0)),
                  pl.BlockSpec((RET_ROWS, RET_DK // 2), lambda b, s: (s, 0)),
                  _resident((RET_HEADS, CHUNK, CHUNK)),
                  _resident((RET_HEADS, CHUNK, RET_DK // 2)),
                  _resident((RET_HEADS, CHUNK, RET_DK // 2)),
                  _resident((RET_HEADS, 1, RET_DV))],
        out_specs=pl.BlockSpec((RET_ROWS, RET_V), lambda b, s: (b * nblk + s, 0)),
        out_shape=jax.ShapeDtypeStruct((t, RET_V), BF16),
        scratch_shapes=[pltpu.VMEM((RET_HEADS, RET_DK, RET_DV), F32)],
        compiler_params=pltpu.CompilerParams(
            dimension_semantics=("arbitrary", "arbitrary"), vmem_limit_bytes=VMEM_LIMIT_BYTES),
        name="retention_core",
    )(proj, cos, sin, dmat, qdec, kdec, cdec)


SB_GROUPS = 8


def _sb_kernel(q_ref, k_ref, v_ref, o_ref, vt_scr, qm_scr, run_scr, acc_scr, *, seq_len):
    nblk = seq_len // CHUNK
    qscale = LOG2E / math.sqrt(SB_DH)
    two = 2 * CHUNK
    groups = range(SB_GROUPS)

    for g in groups:
        for kb in range(nblk):
            vb = v_ref[kb * CHUNK:(kb + 1) * CHUNK, g * LANES:(g + 1) * LANES]
            vt_scr[g, :, kb * CHUNK:(kb + 1) * CHUNK] = vb.astype(F32).T.astype(BF16)

    row = lax.broadcasted_iota(jnp.int32, (CHUNK, CHUNK), 0)
    col = lax.broadcasted_iota(jnp.int32, (CHUNK, CHUNK), 1)
    neg_tri = jnp.where(col >= row, -1.0, 0.0).astype(BF16)
    krow = lax.broadcasted_iota(jnp.int32, (CHUNK, two), 0)
    qcol = lax.broadcasted_iota(jnp.int32, (CHUNK, two), 1) % CHUNK
    causal = krow < qcol
    upper_rows = row < SB_DH

    def tiles(k0s, diag):
        work = [(k0, g) for k0 in k0s for g in groups]
        kbs = [k_ref[pl.ds(k0, CHUNK), g * LANES:(g + 1) * LANES] for k0, g in work]
        zts = [jnp.dot(kb, qm_scr[g], preferred_element_type=F32)
               for kb, (_, g) in zip(kbs, work)]
        ys, tots = [], []
        for kb, zt, (_, g) in zip(kbs, zts, work):
            sp = jnp.maximum(zt, 0.0) + jnp.log(1.0 + jnp.exp2(-jnp.abs(zt))) * LOG2E
            if diag:
                sp = jnp.where(causal, sp, 0.0)
            sp_hi = sp.astype(BF16)
            sp_lo = (sp - sp_hi.astype(F32)).astype(BF16)
            lhs = jnp.concatenate([kb, neg_tri], axis=1)
            rhs = jnp.concatenate([qm_scr[g], sp_hi], axis=0)
            y = (jnp.dot(lhs, rhs, preferred_element_type=F32)
                 + jnp.dot(neg_tri, sp_lo, preferred_element_type=F32))
            ys.append(y)
            tots.append(zt[0:1, :] - y[0:1, :])
        n = len(k0s)
        for g in groups:
            if diag:
                run = tots[g]
                a = jnp.where(causal, jnp.exp2(ys[g]), 0.0).astype(BF16)
                vt = vt_scr[g, :, pl.ds(k0s[0], CHUNK)]
            else:
                run = run_scr[g]
                parts = []
                for t in range(n):
                    parts.append(jnp.exp2(ys[t * SB_GROUPS + g] - run).astype(BF16))
                    run = run + tots[t * SB_GROUPS + g]
                a = parts[0] if n == 1 else jnp.concatenate(parts[::-1], axis=0)
                vt = vt_scr[g, :, pl.ds(k0s[-1], n * CHUNK)]
            pv = jnp.dot(vt, a, preferred_element_type=F32)
            acc_scr[g] = pv if diag else acc_scr[g] + pv
            run_scr[g] = run

    def q_body(qi, _):
        q0 = pl.multiple_of(qi * CHUNK, CHUNK)
        for g in groups:
            qt = q_ref[pl.ds(q0, CHUNK), g * LANES:(g + 1) * LANES].astype(F32).T * qscale
            qm_scr[g] = jnp.concatenate([jnp.where(upper_rows, qt, 0.0),
                                         jnp.where(upper_rows, 0.0, qt)], axis=1).astype(BF16)
        tiles([q0], diag=True)

        def pair_body(p, _):
            k_hi = pl.multiple_of((qi - 1 - 2 * p) * CHUNK, CHUNK)
            tiles([k_hi, k_hi - CHUNK], diag=False)
            return 0

        lax.fori_loop(0, qi // 2, pair_body, 0)

        @pl.when(qi % 2 == 1)
        def _():
            tiles([0], diag=False)

        for g in groups:
            acc = acc_scr[g]
            ot = jnp.concatenate([acc[:SB_DH, :CHUNK], acc[SB_DH:, CHUNK:]], axis=0)
            o_ref[pl.ds(q0, CHUNK), g * LANES:(g + 1) * LANES] = ot.T.astype(o_ref.dtype)
        return 0

    lax.fori_loop(0, nblk, q_body, 0)


def _sb_core(proj, batch, seq_len):
    t = proj.shape[0]
    width = SB_GROUPS * LANES
    n_groups = (SB_HEADS * SB_DH) // width
    kern = functools.partial(_sb_kernel, seq_len=seq_len)
    return pl.pallas_call(
        kern,
        grid=(batch, n_groups),
        in_specs=[pl.BlockSpec((seq_len, width), lambda b, g: (b, g)),
                  pl.BlockSpec((seq_len, width), lambda b, g: (b, n_groups + g)),
                  pl.BlockSpec((seq_len, width), lambda b, g: (b, 2 * n_groups + g))],
        out_specs=pl.BlockSpec((seq_len, width), lambda b, g: (b, g)),
        out_shape=jax.ShapeDtypeStruct((t, SB_HEADS * SB_DH), BF16),
        scratch_shapes=[pltpu.VMEM((SB_GROUPS, LANES, seq_len), BF16),
                        pltpu.VMEM((SB_GROUPS, LANES, 2 * CHUNK), BF16),
                        pltpu.VMEM((SB_GROUPS, 1, 2 * CHUNK), F32),
                        pltpu.VMEM((SB_GROUPS, LANES, 2 * CHUNK), F32)],
        compiler_params=pltpu.CompilerParams(
            dimension_semantics=("arbitrary", "arbitrary"), vmem_limit_bytes=VMEM_LIMIT_BYTES),
        name="sb_core",
    )(proj, proj, proj)


def kernel(x, ret_norm, ret_w_in, ret_w_out, sb_norm, sb_w_in, sb_w_out, ffn_norm,
           ffn_w_up, ffn_conv_w, ffn_conv_b, ffn_w_down, final_norm):
    batch, seq_len, d = x.shape
    assert d == D_MODEL and seq_len % ROW_TILE == 0 and seq_len % RET_ROWS == 0
    xf = x.reshape(batch * seq_len, d)
    for i in range(DEPTH):
        j = i // 2
        if i % 2 == 0:
            proj = _norm_matmul(xf, ret_norm[j], ret_w_in[j].astype(BF16))
            y = _retention_core(proj, batch, seq_len)
            xf = _matmul_residual(y, ret_w_out[j].astype(BF16), xf)
        else:
            proj = _norm_matmul(xf, sb_norm[j], sb_w_in[j].astype(BF16))
            o = _sb_core(proj, batch, seq_len)
            xf = _matmul_residual(o, sb_w_out[j].astype(BF16), xf)
        xf = _conv_ffn(xf, ffn_norm[i], ffn_w_up[i].astype(BF16), ffn_conv_w[i],
                       ffn_conv_b[i], ffn_w_down[i].astype(BF16), final_norm, seq_len,
                       apply_final=(i == DEPTH - 1))
    return xf.reshape(batch, seq_len, d)
```

```python
import functools
import math

import jax
import jax.numpy as jnp
from jax import lax
from jax.experimental import pallas as pl
from jax.experimental.pallas import tpu as pltpu

F32 = jnp.float32
BF16 = jnp.bfloat16

D_MODEL = 1024
DEPTH = 4
EPS = 1e-6
CHUNK = 128
RET_HEADS = 4
RET_DK = 256
RET_DV = 512
RET_QK = RET_HEADS * RET_DK
RET_V = RET_HEADS * RET_DV
RET_IN = 2 * RET_QK + 2 * RET_V
ROPE_BASE = 10000.0
SB_HEADS = 16
SB_DH = 64
SB_IN = 3 * SB_HEADS * SB_DH
D_FF = 2816
CONV_W = 3
LOG2E = 1.4426950408889634

LANES = 128
VMEM_LIMIT_BYTES = 56 * 1024 * 1024
ROW_TILE = 512
N_CHUNK = 512
FF_TILE = 256
HALO = 16


def _resident(shape):
    nd = len(shape)
    return pl.BlockSpec(shape, lambda *_: (0,) * nd, pipeline_mode=pl.Buffered(1))


def _rmsnorm_f32(x, g):
    ms = jnp.mean(x * x, axis=-1, keepdims=True)
    return x * lax.rsqrt(ms + EPS) * g


def _norm_matmul_kernel(x_ref, g_ref, w_ref, o_ref):
    xn = _rmsnorm_f32(x_ref[...], g_ref[...]).astype(BF16)
    n_out = o_ref.shape[1]
    for n0 in range(0, n_out, N_CHUNK):
        o_ref[:, n0:n0 + N_CHUNK] = jnp.dot(
            xn, w_ref[:, n0:n0 + N_CHUNK], preferred_element_type=F32).astype(o_ref.dtype)


def _norm_matmul(x, g, w):
    t, d = x.shape
    n = w.shape[1]
    return pl.pallas_call(
        _norm_matmul_kernel,
        grid=(t // ROW_TILE,),
        in_specs=[pl.BlockSpec((ROW_TILE, d), lambda i: (i, 0)),
                  _resident((1, d)),
                  _resident((d, n))],
        out_specs=pl.BlockSpec((ROW_TILE, n), lambda i: (i, 0)),
        out_shape=jax.ShapeDtypeStruct((t, n), BF16),
        compiler_params=pltpu.CompilerParams(
            dimension_semantics=("arbitrary",), vmem_limit_bytes=VMEM_LIMIT_BYTES),
        name="norm_matmul",
    )(x, g.reshape(1, d), w)


def _matmul_residual_kernel(y_ref, w_ref, x_ref, o_ref):
    o_ref[...] = x_ref[...] + jnp.dot(y_ref[...], w_ref[...], preferred_element_type=F32)


def _matmul_residual(y, w, x):
    t, k = y.shape
    d = w.shape[1]
    return pl.pallas_call(
        _matmul_residual_kernel,
        grid=(t // ROW_TILE,),
        in_specs=[pl.BlockSpec((ROW_TILE, k), lambda i: (i, 0)),
                  _resident((k, d)),
                  pl.BlockSpec((ROW_TILE, d), lambda i: (i, 0))],
        out_specs=pl.BlockSpec((ROW_TILE, d), lambda i: (i, 0)),
        out_shape=jax.ShapeDtypeStruct((t, d), F32),
        compiler_params=pltpu.CompilerParams(
            dimension_semantics=("arbitrary",), vmem_limit_bytes=VMEM_LIMIT_BYTES),
        name="matmul_residual",
    )(y, w, x)


def _ffn_kernel(x_ref, halo_ref, g_ref, wup_ref, cw_ref, cb_ref, wdn_ref, fg_ref, o_ref,
                h_scr, *, tiles_per_seq, apply_final):
    i = pl.program_id(0)
    x = x_ref[...]
    g = g_ref[...]
    xn = _rmsnorm_f32(x, g).astype(BF16)
    first = (i % tiles_per_seq) == 0
    hn = _rmsnorm_f32(halo_ref[...], g)
    hn = jnp.where(first, 0.0, hn).astype(BF16)
    xe = jnp.concatenate([hn, xn], axis=0)

    def conv_cols(c0):
        u = jnp.dot(xe, wup_ref[:, c0:c0 + FF_TILE], preferred_element_type=F32)
        w = cw_ref[:, c0:c0 + FF_TILE]
        out = (u[HALO:] * w[2:3]
               + pltpu.roll(u, 1, 0)[HALO:] * w[1:2]
               + pltpu.roll(u, 2, 0)[HALO:] * w[0:1])
        return out + cb_ref[:, c0:c0 + FF_TILE]

    for f0 in range(0, D_FF, FF_TILE):
        gate = conv_cols(f0)
        val = conv_cols(D_FF + f0)
        h_scr[:, f0:f0 + FF_TILE] = (gate * jax.nn.sigmoid(gate) * val).astype(BF16)

    out = x + jnp.dot(h_scr[...], wdn_ref[...], preferred_element_type=F32)
    if apply_final:
        out = _rmsnorm_f32(out, fg_ref[...])
    o_ref[...] = out


def _conv_ffn(x, g, w_up, conv_w, conv_b, w_down, final_g, seq_len, apply_final):
    t, d = x.shape
    tm = ROW_TILE
    kern = functools.partial(_ffn_kernel, tiles_per_seq=seq_len // tm, apply_final=apply_final)
    halo_blocks_per_tile = tm // HALO
    return pl.pallas_call(
        kern,
        grid=(t // tm,),
        in_specs=[pl.BlockSpec((tm, d), lambda i: (i, 0)),
                  pl.BlockSpec((HALO, d),
                               lambda i: (jnp.maximum(i * halo_blocks_per_tile - 1, 0), 0)),
                  _resident((1, d)),
                  _resident((d, 2 * D_FF)),
                  _resident((CONV_W, 2 * D_FF)),
                  _resident((1, 2 * D_FF)),
                  _resident((D_FF, d)),
                  _resident((1, d))],
        out_specs=pl.BlockSpec((tm, d), lambda i: (i, 0)),
        out_shape=jax.ShapeDtypeStruct((t, d), F32),
        scratch_shapes=[pltpu.VMEM((tm, D_FF), BF16)],
        compiler_params=pltpu.CompilerParams(
            dimension_semantics=("arbitrary",), vmem_limit_bytes=VMEM_LIMIT_BYTES),
        name="conv_ffn",
    )(x, x, g.reshape(1, d), w_up, conv_w, conv_b.reshape(1, 2 * D_FF), w_down,
      final_g.reshape(1, d))


RET_ROWS = 512


def _retention_kernel(p_ref, cos_ref, sin_ref, dmat_ref, qdec_ref, kdec_ref, cdec_ref,
                      o_ref, state_scr):
    @pl.when(pl.program_id(1) == 0)
    def _():
        state_scr[...] = jnp.zeros_like(state_scr)

    half = RET_DK // 2

    def rotary(a, cos, sin):
        a1, a2 = a[:, :half], a[:, half:]
        return a1 * cos - a2 * sin, a1 * sin + a2 * cos

    for c in range(RET_ROWS // CHUNK):
        rows = slice(c * CHUNK, (c + 1) * CHUNK)
        cos = cos_ref[rows, :]
        sin = sin_ref[rows, :]
        for h in range(RET_HEADS):
            q = p_ref[rows, h * RET_DK:(h + 1) * RET_DK].astype(F32)
            k = p_ref[rows, RET_QK + h * RET_DK:RET_QK + (h + 1) * RET_DK].astype(F32)
            v = p_ref[rows, 2 * RET_QK + h * RET_DV:2 * RET_QK + (h + 1) * RET_DV]
            gate = p_ref[rows, 2 * RET_QK + RET_V + h * RET_DV:
                         2 * RET_QK + RET_V + (h + 1) * RET_DV].astype(F32)
            q1, q2 = rotary(q, cos, sin)
            k1, k2 = rotary(k, cos, sin)
            kscale = RET_DK ** -0.5
            k1 = k1 * kscale
            k2 = k2 * kscale
            qr = jnp.concatenate([q1, q2], axis=1).astype(BF16)
            kr = jnp.concatenate([k1, k2], axis=1).astype(BF16)
            qd = jnp.concatenate([q1 * qdec_ref[h], q2 * qdec_ref[h]], axis=1).astype(BF16)
            kd = jnp.concatenate([k1 * kdec_ref[h], k2 * kdec_ref[h]], axis=1).astype(BF16)

            scores = lax.dot_general(qr, kr, (((1,), (1,)), ((), ())),
                                     preferred_element_type=F32) * dmat_ref[h]
            st = state_scr[h]
            o = (jnp.dot(scores.astype(BF16), v, preferred_element_type=F32)
                 + jnp.dot(qd, st.astype(BF16), preferred_element_type=F32))
            state_scr[h] = st * cdec_ref[h] + lax.dot_general(
                kd, v, (((0,), (0,)), ((), ())), preferred_element_type=F32)

            o = o * lax.rsqrt(jnp.mean(o * o, axis=-1, keepdims=True) + EPS)
            y = gate * jax.nn.sigmoid(gate) * o
            o_ref[rows, h * RET_DV:(h + 1) * RET_DV] = y.astype(o_ref.dtype)


def _retention_tables(seq_len):
    pos = jnp.arange(seq_len, dtype=F32)
    inv_freq = ROPE_BASE ** (-jnp.arange(0, RET_DK, 2, dtype=F32) / RET_DK)
    ang = pos[:, None] * inv_freq[None, :]
    cos, sin = jnp.cos(ang), jnp.sin(ang)
    log_g = jnp.log(1.0 - 2.0 ** (-5.0 - jnp.arange(RET_HEADS, dtype=F32)))
    cpos = jnp.arange(CHUNK, dtype=F32)
    diff = cpos[:, None] - cpos[None, :]
    dmat = jnp.where(diff[None] >= 0,
                     jnp.exp(jnp.maximum(diff, 0.0)[None] * log_g[:, None, None]), 0.0)
    qdec = jnp.exp((cpos + 1.0)[None, :] * log_g[:, None])
    kdec = jnp.exp((CHUNK - 1.0 - cpos)[None, :] * log_g[:, None])
    cdec = jnp.exp(CHUNK * log_g)
    qdec = jnp.broadcast_to(qdec[:, :, None], (RET_HEADS, CHUNK, RET_DK // 2))
    kdec = jnp.broadcast_to(kdec[:, :, None], (RET_HEADS, CHUNK, RET_DK // 2))
    cdec = jnp.broadcast_to(cdec[:, None, None], (RET_HEADS, 1, RET_DV))
    return cos, sin, dmat, qdec, kdec, cdec


def _retention_core(proj, batch, seq_len):
    t = proj.shape[0]
    cos, sin, dmat, qdec, kdec, cdec = _retention_tables(seq_len)
    nblk = seq_len // RET_ROWS
    return pl.pallas_call(
        _retention_kernel,
        grid=(batch, nblk),
        in_specs=[pl.BlockSpec((RET_ROWS, RET_IN), lambda b, s: (b * nblk + s, 0)),
                  pl.BlockSpec((RET_ROWS, RET_DK // 2), lambda b, s: (s, 0)),
                  pl.BlockSpec((RET_ROWS, RET_DK // 2), lambda b, s: (s, 0)),
                  _resident((RET_HEADS, CHUNK, CHUNK)),
                  _resident((RET_HEADS, CHUNK, RET_DK // 2)),
                  _resident((RET_HEADS, CHUNK, RET_DK // 2)),
                  _resident((RET_HEADS, 1, RET_DV))],
        out_specs=pl.BlockSpec((RET_ROWS, RET_V), lambda b, s: (b * nblk + s, 0)),
        out_shape=jax.ShapeDtypeStruct((t, RET_V), BF16),
        scratch_shapes=[pltpu.VMEM((RET_HEADS, RET_DK, RET_DV), F32)],
        compiler_params=pltpu.CompilerParams(
            dimension_semantics=("arbitrary", "arbitrary"), vmem_limit_bytes=VMEM_LIMIT_BYTES),
        name="retention_core",
    )(proj, cos, sin, dmat, qdec, kdec, cdec)


SB_GROUPS = 8


def _sb_kernel(q_ref, k_ref, v_ref, o_ref, vt_scr, qm_scr, run_scr, acc_scr, *, seq_len):
    nblk = seq_len // CHUNK
    qscale = LOG2E / math.sqrt(SB_DH)
    two = 2 * CHUNK
    groups = range(SB_GROUPS)

    for g in groups:
        for kb in range(nblk):
            vb = v_ref[kb * CHUNK:(kb + 1) * CHUNK, g * LANES:(g + 1) * LANES]
            vt_scr[g, :, kb * CHUNK:(kb + 1) * CHUNK] = vb.astype(F32).T.astype(BF16)

    row = lax.broadcasted_iota(jnp.int32, (CHUNK, CHUNK), 0)
    col = lax.broadcasted_iota(jnp.int32, (CHUNK, CHUNK), 1)
    neg_tri = jnp.where(col >= row, -1.0, 0.0).astype(BF16)
    krow = lax.broadcasted_iota(jnp.int32, (CHUNK, two), 0)
    qcol = lax.broadcasted_iota(jnp.int32, (CHUNK, two), 1) % CHUNK
    causal = krow < qcol
    upper_rows = row < SB_DH

    def tiles(k0s, diag):
        n = len(k0s)
        kbs = [[k_ref[pl.ds(k0, CHUNK), g * LANES:(g + 1) * LANES] for g in groups] for k0 in k0s]

        def scores(t):
            return [jnp.dot(kbs[t][g], qm_scr[g], preferred_element_type=F32) for g in groups]

        def suffix(t, g, zt):
            sp = jnp.maximum(zt, 0.0) + jnp.log(1.0 + jnp.exp2(-jnp.abs(zt))) * LOG2E
            if diag and t == 0:
                sp = jnp.where(causal, sp, 0.0)
            lhs = jnp.concatenate([kbs[t][g], neg_tri], axis=1)
            rhs = jnp.concatenate([qm_scr[g], sp.astype(BF16)], axis=0)
            y = jnp.dot(lhs, rhs, preferred_element_type=F32)
            return y, zt[0:1, :] - y[0:1, :]

        outs = []
        zts = scores(0)
        for t in range(n):
            nxt = []
            row_out = []
            for g in groups:
                row_out.append(suffix(t, g, zts[g]))
                if t + 1 < n:
                    nxt.append(jnp.dot(kbs[t + 1][g], qm_scr[g], preferred_element_type=F32))
            outs.append(row_out)
            zts = nxt
        for g in groups:
            run = None if diag else run_scr[g]
            parts = []
            for t in range(n):
                y, tot = outs[t][g]
                if diag and t == 0:
                    parts.append(jnp.where(causal, jnp.exp2(y), 0.0).astype(BF16))
                    run = tot
                else:
                    parts.append(jnp.exp2(y - run).astype(BF16))
                    run = run + tot
            a = parts[0] if n == 1 else jnp.concatenate(parts[::-1], axis=0)
            pv = jnp.dot(vt_scr[g, :, pl.ds(k0s[-1], n * CHUNK)], a,
                         preferred_element_type=F32)
            acc_scr[g] = pv if diag else acc_scr[g] + pv
            run_scr[g] = run

    def q_body(qi, _):
        q0 = pl.multiple_of(qi * CHUNK, CHUNK)
        for g in groups:
            qt = q_ref[pl.ds(q0, CHUNK), g * LANES:(g + 1) * LANES].astype(F32).T * qscale
            qm_scr[g] = jnp.concatenate([jnp.where(upper_rows, qt, 0.0),
                                         jnp.where(upper_rows, 0.0, qt)], axis=1).astype(BF16)
        odd = qi % 2

        @pl.when(odd == 0)
        def _():
            tiles([q0], diag=True)

        @pl.when(odd == 1)
        def _():
            tiles([q0, q0 - CHUNK], diag=True)

        def pair_body(p, _):
            k_hi = pl.multiple_of((qi - 1 - odd - 2 * p) * CHUNK, CHUNK)
            tiles([k_hi, k_hi - CHUNK], diag=False)
            return 0

        lax.fori_loop(0, (qi - odd) // 2, pair_body, 0)

        for g in groups:
            acc = acc_scr[g]
            ot = jnp.concatenate([acc[:SB_DH, :CHUNK], acc[SB_DH:, CHUNK:]], axis=0)
            o_ref[pl.ds(q0, CHUNK), g * LANES:(g + 1) * LANES] = ot.T.astype(o_ref.dtype)
        return 0

    lax.fori_loop(0, nblk, q_body, 0)


def _sb_core(proj, batch, seq_len):
    t = proj.shape[0]
    width = SB_GROUPS * LANES
    n_groups = (SB_HEADS * SB_DH) // width
    kern = functools.partial(_sb_kernel, seq_len=seq_len)
    return pl.pallas_call(
        kern,
        grid=(batch, n_groups),
        in_specs=[pl.BlockSpec((seq_len, width), lambda b, g: (b, g)),
                  pl.BlockSpec((seq_len, width), lambda b, g: (b, n_groups + g)),
                  pl.BlockSpec((seq_len, width), lambda b, g: (b, 2 * n_groups + g))],
        out_specs=pl.BlockSpec((seq_len, width), lambda b, g: (b, g)),
        out_shape=jax.ShapeDtypeStruct((t, SB_HEADS * SB_DH), BF16),
        scratch_shapes=[pltpu.VMEM((SB_GROUPS, LANES, seq_len), BF16),
                        pltpu.VMEM((SB_GROUPS, LANES, 2 * CHUNK), BF16),
                        pltpu.VMEM((SB_GROUPS, 1, 2 * CHUNK), F32),
                        pltpu.VMEM((SB_GROUPS, LANES, 2 * CHUNK), F32)],
        compiler_params=pltpu.CompilerParams(
            dimension_semantics=("arbitrary", "arbitrary"), vmem_limit_bytes=VMEM_LIMIT_BYTES),
        name="sb_core",
    )(proj, proj, proj)


def kernel(x, ret_norm, ret_w_in, ret_w_out, sb_norm, sb_w_in, sb_w_out, ffn_norm,
           ffn_w_up, ffn_conv_w, ffn_conv_b, ffn_w_down, final_norm):
    batch, seq_len, d = x.shape
    assert d == D_MODEL and seq_len % ROW_TILE == 0 and seq_len % RET_ROWS == 0
    xf = x.reshape(batch * seq_len, d)
    for i in range(DEPTH):
        j = i // 2
        if i % 2 == 0:
            proj = _norm_matmul(xf, ret_norm[j], ret_w_in[j].astype(BF16))
            y = _retention_core(proj, batch, seq_len)
            xf = _matmul_residual(y, ret_w_out[j].astype(BF16), xf)
        else:
            proj = _norm_matmul(xf, sb_norm[j], sb_w_in[j].astype(BF16))
            o = _sb_core(proj, batch, seq_len)
            xf = _matmul_residual(o, sb_w_out[j].astype(BF16), xf)
        xf = _conv_ffn(xf, ffn_norm[i], ffn_w_up[i].astype(BF16), ffn_conv_w[i],
                       ffn_conv_b[i], ffn_w_down[i].astype(BF16), final_norm, seq_len,
                       apply_final=(i == DEPTH - 1))
    return xf.reshape(batch, seq_len, d)
```
